```python
import jax, jax.numpy as jnp
from jax import lax
import numpy as np

D_MODEL = 1024
BATCH = 8
SEQ = 4096
DEPTH = 1

D_PLE = 256
D_RNN = 1024
RNN_BLOCKS = 8
RNN_BW = D_RNN // RNN_BLOCKS
RNN_CONV = 4
LRU_C = 8.0
D_CONV = 1024
SHORT_CONV = 3
N_BRANCH = 2
D_FF = 3 * D_MODEL
FFN_CONV = 3
D_IN = 2 * D_RNN + 3 * D_CONV + N_BRANCH * D_MODEL
EPS = 1e-6

kernel_name = "hybrid_rglru_shortconv_block"


def rmsnorm(u, g):
    uf = u.astype(jnp.float32)
    y = uf * lax.rsqrt(jnp.mean(uf * uf, axis=-1, keepdims=True) + EPS)
    return (y * g.astype(jnp.float32)).astype(u.dtype)


def causal_dwconv(u, w):
    K = w.shape[0]
    S = u.shape[1]
    up = jnp.pad(u, ((0, 0), (K - 1, 0), (0, 0)))
    y = up[:, 0:S] * w[0]
    for k in range(1, K):
        y = y + up[:, k:k + S] * w[k]
    return y


def rg_lru(xc, w_a, b_a, w_x, b_x, lam):
    Bn, S, _ = xc.shape
    xb = xc.reshape(Bn, S, RNN_BLOCKS, RNN_BW)
    r = jax.nn.sigmoid((jnp.einsum('bshi,hij->bshj', xb, w_a).reshape(Bn, S, D_RNN) + b_a).astype(jnp.float32))
    i = jax.nn.sigmoid((jnp.einsum('bshi,hij->bshj', xb, w_x).reshape(Bn, S, D_RNN) + b_x).astype(jnp.float32))
    log_a = -LRU_C * r * jax.nn.softplus(-lam.astype(jnp.float32))
    a = jnp.exp(log_a)
    mult = jnp.sqrt(-jnp.expm1(2.0 * log_a))
    b = mult * (i * xc.astype(jnp.float32))

    def combine(lhs, rhs):
        a1, b1 = lhs
        a2, b2 = rhs
        return a1 * a2, a2 * b1 + b2

    _, h = lax.associative_scan(combine, (a, b), axis=1)
    return h.astype(xc.dtype)


def setup_inputs(seed: int = 0) -> dict:
    key = jax.random.key(seed)
    ks = jax.random.split(key, 24)
    f32 = jnp.float32

    def nrm(k, shape, scale):
        return jax.random.normal(k, shape, f32) * scale

    u = jax.random.uniform(ks[10], (DEPTH, D_RNN), f32, 0.9, 0.999)
    a0 = u ** (1.0 / LRU_C)
    lru_lambda = jnp.log(a0) - jnp.log1p(-a0)
    return {
        "x": nrm(ks[0], (BATCH, SEQ, D_MODEL), 1.0),
        "p": nrm(ks[1], (DEPTH, BATCH, SEQ, D_PLE), 1.0),
        "g_mix": 1.0 + nrm(ks[2], (DEPTH, D_MODEL), 0.02),
        "w_in": nrm(ks[3], (DEPTH, D_MODEL, D_IN), D_MODEL ** -0.5),
        "rnn_conv_w": nrm(ks[4], (DEPTH, RNN_CONV, D_RNN), RNN_CONV ** -0.5),
        "rnn_conv_b": nrm(ks[5], (DEPTH, D_RNN), 0.01),
        "w_rg_a": nrm(ks[6], (DEPTH, RNN_BLOCKS, RNN_BW, RNN_BW), RNN_BW ** -0.5),
        "b_rg_a": nrm(ks[7], (DEPTH, D_RNN), 0.01),
        "w_rg_x": nrm(ks[8], (DEPTH, RNN_BLOCKS, RNN_BW, RNN_BW), RNN_BW ** -0.5),
        "b_rg_x": nrm(ks[9], (DEPTH, D_RNN), 0.01),
        "lru_lambda": lru_lambda,
        "sc_conv_w": nrm(ks[11], (DEPTH, SHORT_CONV, D_CONV), SHORT_CONV ** -0.5),
        "w_proj_a": nrm(ks[12], (DEPTH, D_RNN, D_MODEL), D_RNN ** -0.5),
        "w_proj_b": nrm(ks[13], (DEPTH, D_CONV, D_MODEL), D_CONV ** -0.5),
        "w_out": nrm(ks[14], (DEPTH, D_MODEL, D_MODEL), D_MODEL ** -0.5),
        "g_ffn": 1.0 + nrm(ks[15], (DEPTH, D_MODEL), 0.02),
        "w_up": nrm(ks[16], (DEPTH, D_MODEL, 2 * D_FF), D_MODEL ** -0.5),
        "ffn_conv_w": nrm(ks[17], (DEPTH, FFN_CONV, 2 * D_FF), FFN_CONV ** -0.5),
        "ffn_conv_b": nrm(ks[18], (DEPTH, 2 * D_FF), 0.01),
        "w_down": nrm(ks[19], (DEPTH, D_FF, D_MODEL), D_FF ** -0.5),
        "w_ple_gate": nrm(ks[20], (DEPTH, D_MODEL, D_MODEL), D_MODEL ** -0.5),
        "w_ple_proj": nrm(ks[21], (DEPTH, D_PLE, D_MODEL), D_PLE ** -0.5),
        "g_ple": 1.0 + nrm(ks[22], (DEPTH, D_MODEL), 0.02),
        "g_final": 1.0 + nrm(ks[23], (D_MODEL,), 0.02),
    }


def reference(x, p, g_mix, w_in, rnn_conv_w, rnn_conv_b, w_rg_a, b_rg_a, w_rg_x, b_rg_x,
              lru_lambda, sc_conv_w, w_proj_a, w_proj_b, w_out, g_ffn, w_up, ffn_conv_w,
              ffn_conv_b, w_down, w_ple_gate, w_ple_proj, g_ple, g_final):
    split_idx = (D_RNN, 2 * D_RNN, 2 * D_RNN + D_CONV, 2 * D_RNN + 2 * D_CONV,
                 2 * D_RNN + 3 * D_CONV, 2 * D_RNN + 3 * D_CONV + D_MODEL)
    for l in range(DEPTH):
        h = rmsnorm(x, g_mix[l])
        z = h @ w_in[l]
        xr, gr, cb, cc, cx, ga, gb = jnp.split(z, split_idx, axis=-1)

        xr = causal_dwconv(xr, rnn_conv_w[l]) + rnn_conv_b[l]
        ya = rg_lru(xr, w_rg_a[l], b_rg_a[l], w_rg_x[l], b_rg_x[l], lru_lambda[l]) * jax.nn.gelu(gr)

        yb = cb * causal_dwconv(cc * cx, sc_conv_w[l])

        m = jax.nn.sigmoid(ga) * (ya @ w_proj_a[l]) + jax.nn.sigmoid(gb) * (yb @ w_proj_b[l])
        x = x + m @ w_out[l]

        h = rmsnorm(x, g_ffn[l])
        u = causal_dwconv(h @ w_up[l], ffn_conv_w[l]) + ffn_conv_b[l]
        ug, uv = jnp.split(u, 2, axis=-1)
        x = x + (jax.nn.gelu(ug) * uv) @ w_down[l]

        e = rmsnorm(p[l] @ w_ple_proj[l], g_ple[l])
        x = x + jax.nn.sigmoid(x @ w_ple_gate[l]) * e
    return rmsnorm(x, g_final)
```

```python
import functools

import jax
import jax.numpy as jnp
from jax import lax
from jax.experimental import pallas as pl
from jax.experimental.pallas import tpu as pltpu

D_MODEL = 1024
BATCH = 8
SEQ = 4096
D_PLE = 256
D_RNN = 1024
RNN_BLOCKS = 8
RNN_BW = D_RNN // RNN_BLOCKS
RNN_CONV = 4
LRU_C = 8.0
D_CONV = 1024
SHORT_CONV = 3
D_FF = 3 * D_MODEL
FFN_CONV = 3
EPS = 1e-6

SUBLANES = 8
assert BATCH == SUBLANES, "time-major layout puts one timestep's batch on the sublanes"

VMEM_LIMIT_BYTES = 60000 * 1024

TS = 64
TM = TS * BATCH
FF_CHUNK = 512

F32 = jnp.float32
BF16 = jnp.bfloat16


def _rmsnorm(u, g):
    y = u * lax.rsqrt(jnp.mean(u * u, axis=-1, keepdims=True) + EPS)
    return y * g


def _causal_conv(halo_ref, cols, pre, w_ref, taps):
    hrows = (taps - 1) * BATCH
    rows = pre.shape[0]
    ext = jnp.concatenate([halo_ref[:, cols], pre], axis=0)
    halo_ref[:, cols] = pre[rows - hrows:, :]
    y = ext[0:rows, :] * w_ref[0:1, cols]
    for k in range(1, taps):
        y = y + ext[k * BATCH:k * BATCH + rows, :] * w_ref[k:k + 1, cols]
    return y


def _mix_kernel(x_ref, g_ref, win_ref, cw_ref, cb_ref, wrg_ref, ba_ref, bx_ref, lam_ref,
                scw_ref, wpa_ref, wpb_ref, wout_ref, o_ref,
                xr_halo, v_halo, h_state, a_buf, b_buf):
    @pl.when(pl.program_id(0) == 0)
    def _():
        xr_halo[...] = jnp.zeros_like(xr_halo)
        v_halo[...] = jnp.zeros_like(v_halo)
        h_state[...] = jnp.zeros_like(h_state)

    x = x_ref[...]
    h = _rmsnorm(x, g_ref[...]).astype(BF16)

    def proj(k):
        return jnp.dot(h, win_ref[:, k * D_MODEL:(k + 1) * D_MODEL], preferred_element_type=F32)

    all_cols = slice(0, D_RNN)

    xr = _causal_conv(xr_halo, all_cols, proj(0), cw_ref, RNN_CONV) + cb_ref[...]
    xr_bf = xr.astype(BF16)
    neg_c_softplus = -LRU_C * jax.nn.softplus(-lam_ref[...])
    for blk in range(RNN_BLOCKS):
        cols = slice(blk * RNN_BW, (blk + 1) * RNN_BW)
        gates = jnp.dot(xr_bf[:, cols], wrg_ref[blk], preferred_element_type=F32)
        r = jax.nn.sigmoid(gates[:, :RNN_BW] + ba_ref[:, cols])
        i = jax.nn.sigmoid(gates[:, RNN_BW:] + bx_ref[:, cols])
        log_a = r * neg_c_softplus[:, cols]
        a = jnp.exp(log_a)
        one_minus_a2 = -jnp.tanh(log_a) * (1.0 + a * a)
        a_buf[:, cols] = a
        b_buf[:, cols] = jnp.sqrt(one_minus_a2) * (i * xr[:, cols])

    def scan_step(t, hc):
        rows = pl.ds(pl.multiple_of(t * BATCH, BATCH), BATCH)
        hn = a_buf[rows, :] * hc + b_buf[rows, :]
        a_buf[rows, :] = hn
        return hn

    h_state[...] = lax.fori_loop(0, TS, scan_step, h_state[...], unroll=8)
    ya = a_buf[...] * jax.nn.gelu(proj(1))
    pa = jnp.dot(ya.astype(BF16), wpa_ref[...], preferred_element_type=F32)

    cb_gate = proj(2)
    v = proj(3) * proj(4)
    yb = cb_gate * _causal_conv(v_halo, all_cols, v, scw_ref, SHORT_CONV)
    pb = jnp.dot(yb.astype(BF16), wpb_ref[...], preferred_element_type=F32)

    m = jax.nn.sigmoid(proj(5)) * pa + jax.nn.sigmoid(proj(6)) * pb
    o_ref[...] = x + jnp.dot(m.astype(BF16), wout_ref[...], preferred_element_type=F32)


def _ffn_kernel(x_ref, p_ref, g_ref, wup_ref, fw_ref, fb_ref, wdn_ref, wpg_ref, wpp_ref,
                gple_ref, gfin_ref, o_ref, up_halo, *, final_norm):
    @pl.when(pl.program_id(0) == 0)
    def _():
        up_halo[...] = jnp.zeros_like(up_halo)

    x = x_ref[...]
    h = _rmsnorm(x, g_ref[...]).astype(BF16)

    def conv_up(cols):
        pre = jnp.dot(h, wup_ref[:, cols], preferred_element_type=F32)
        return _causal_conv(up_halo, cols, pre, fw_ref, FFN_CONV) + fb_ref[:, cols]

    acc = x
    for c in range(D_FF // FF_CHUNK):
        ug = conv_up(slice(c * FF_CHUNK, (c + 1) * FF_CHUNK))
        uv = conv_up(slice(D_FF + c * FF_CHUNK, D_FF + (c + 1) * FF_CHUNK))
        act = (jax.nn.gelu(ug) * uv).astype(BF16)
        acc = acc + jnp.dot(act, wdn_ref[c * FF_CHUNK:(c + 1) * FF_CHUNK, :],
                            preferred_element_type=F32)
    x = acc

    e = jnp.dot(p_ref[...].astype(BF16), wpp_ref[...], preferred_element_type=F32)
    e = _rmsnorm(e, gple_ref[...])
    gate = jax.nn.sigmoid(jnp.dot(x.astype(BF16), wpg_ref[...], preferred_element_type=F32))
    x = x + gate * e
    if final_norm:
        x = _rmsnorm(x, gfin_ref[...])
    o_ref[...] = x


def _resident(shape):
    nd = len(shape)
    return pl.BlockSpec(shape, lambda i: (0,) * nd, pipeline_mode=pl.Buffered(1))


def _rows(width):
    return pl.BlockSpec((TM, width), lambda i: (i, 0))


_PARAMS = pltpu.CompilerParams(dimension_semantics=("arbitrary",),
                               vmem_limit_bytes=VMEM_LIMIT_BYTES)


def _mix_call(xt, g, win, cw, cb, wrg, ba, bx, lam, scw, wpa, wpb, wout):
    n_rows = xt.shape[0]
    consts = (g, win, cw, cb, wrg, ba, bx, lam, scw, wpa, wpb, wout)
    return pl.pallas_call(
        _mix_kernel,
        grid=(n_rows // TM,),
        in_specs=[_rows(D_MODEL)] + [_resident(c.shape) for c in consts],
        out_specs=_rows(D_MODEL),
        out_shape=jax.ShapeDtypeStruct((n_rows, D_MODEL), F32),
        scratch_shapes=[
            pltpu.VMEM(((RNN_CONV - 1) * BATCH, D_RNN), F32),
            pltpu.VMEM(((SHORT_CONV - 1) * BATCH, D_CONV), F32),
            pltpu.VMEM((BATCH, D_RNN), F32),
            pltpu.VMEM((TM, D_RNN), F32),
            pltpu.VMEM((TM, D_RNN), F32),
        ],
        compiler_params=_PARAMS,
        name="mix",
    )(xt, *consts)


def _ffn_call(xt, pt, g, wup, fw, fb, wdn, wpg, wpp, gple, gfin, final_norm):
    n_rows = xt.shape[0]
    consts = (g, wup, fw, fb, wdn, wpg, wpp, gple, gfin)
    return pl.pallas_call(
        functools.partial(_ffn_kernel, final_norm=final_norm),
        grid=(n_rows // TM,),
        in_specs=[_rows(D_MODEL), _rows(D_PLE)] + [_resident(c.shape) for c in consts],
        out_specs=_rows(D_MODEL),
        out_shape=jax.ShapeDtypeStruct((n_rows, D_MODEL), F32),
        scratch_shapes=[pltpu.VMEM(((FFN_CONV - 1) * BATCH, 2 * D_FF), F32)],
        compiler_params=_PARAMS,
        name="ffn",
    )(xt, pt, *consts)


def _time_major(a):
    b, s, d = a.shape
    return a.transpose(1, 0, 2).reshape(s * b, d)


def kernel(x, p, g_mix, w_in, rnn_conv_w, rnn_conv_b, w_rg_a, b_rg_a, w_rg_x, b_rg_x, lru_lambda, sc_conv_w, w_proj_a, w_proj_b, w_out, g_ffn, w_up, ffn_conv_w, ffn_conv_b, w_down, w_ple_gate, w_ple_proj, g_ple, g_final):
    depth = w_in.shape[0]
    bsz, seq, d = x.shape
    assert (bsz, seq, d) == (BATCH, SEQ, D_MODEL) and seq % TS == 0

    def row(v):
        return v.reshape(1, -1)

    xt = _time_major(x)
    for l in range(depth):
        w_rg = jnp.concatenate([w_rg_a[l], w_rg_x[l]], axis=-1).astype(BF16)
        xt = _mix_call(xt, row(g_mix[l]), w_in[l].astype(BF16), rnn_conv_w[l], row(rnn_conv_b[l]),
                       w_rg, row(b_rg_a[l]), row(b_rg_x[l]), row(lru_lambda[l]), sc_conv_w[l],
                       w_proj_a[l].astype(BF16), w_proj_b[l].astype(BF16), w_out[l].astype(BF16))
        xt = _ffn_call(xt, _time_major(p[l]), row(g_ffn[l]), w_up[l].astype(BF16), ffn_conv_w[l],
                       row(ffn_conv_b[l]), w_down[l].astype(BF16), w_ple_gate[l].astype(BF16),
                       w_ple_proj[l].astype(BF16), row(g_ple[l]), row(g_final),
                       final_norm=(l == depth - 1))
    return xt.reshape(seq, bsz, d).transpose(1, 0, 2)
```

```python
import functools

import jax
import jax.numpy as jnp
from jax import lax
from jax.experimental import pallas as pl
from jax.experimental.pallas import tpu as pltpu

D_MODEL = 1024
BATCH = 8
SEQ = 4096
D_PLE = 256
D_RNN = 1024
RNN_BLOCKS = 8
RNN_BW = D_RNN // RNN_BLOCKS
RNN_CONV = 4
LRU_C = 8.0
D_CONV = 1024
SHORT_CONV = 3
D_FF = 3 * D_MODEL
FFN_CONV = 3
EPS = 1e-6

SUBLANES = 8
assert BATCH == SUBLANES, "time-major layout puts one timestep's batch on the sublanes"

VMEM_LIMIT_BYTES = 60000 * 1024

TS = 64
TM = TS * BATCH
FF_CHUNK = 512

F32 = jnp.float32
BF16 = jnp.bfloat16


def _rmsnorm(u, g):
    y = u * lax.rsqrt(jnp.mean(u * u, axis=-1, keepdims=True) + EPS)
    return y * g


def _causal_conv(halo_ref, cols, pre, w_ref, taps):
    hrows = (taps - 1) * BATCH
    rows = pre.shape[0]
    ext = jnp.concatenate([halo_ref[:, cols], pre], axis=0)
    halo_ref[:, cols] = pre[rows - hrows:, :]
    y = ext[0:rows, :] * w_ref[0:1, cols]
    for k in range(1, taps):
        y = y + ext[k * BATCH:k * BATCH + rows, :] * w_ref[k:k + 1, cols]
    return y


def _mix_kernel(x_ref, g_ref, win_ref, cw_ref, cb_ref, wrg_ref, ba_ref, bx_ref, lam_ref,
                scw_ref, wpa_ref, wpb_ref, wout_ref, o_ref,
                xr_halo, v_halo, h_state, a_buf, b_buf, *, batch_major_in):
    @pl.when(pl.program_id(0) == 0)
    def _():
        xr_halo[...] = jnp.zeros_like(xr_halo)
        v_halo[...] = jnp.zeros_like(v_halo)
        h_state[...] = jnp.zeros_like(h_state)

    x = _to_time_major(x_ref[...]) if batch_major_in else x_ref[...]
    h = _rmsnorm(x, g_ref[...]).astype(BF16)

    def proj(k):
        return jnp.dot(h, win_ref[:, k * D_MODEL:(k + 1) * D_MODEL], preferred_element_type=F32)

    all_cols = slice(0, D_RNN)


    xr_pre = proj(0)
    cb_gate = proj(2)
    xr = _causal_conv(xr_halo, all_cols, xr_pre, cw_ref, RNN_CONV) + cb_ref[...]
    xr_bf = xr.astype(BF16)
    gates = [jnp.dot(xr_bf[:, blk * RNN_BW:(blk + 1) * RNN_BW], wrg_ref[blk],
                     preferred_element_type=F32) for blk in range(RNN_BLOCKS)]
    v = proj(3) * proj(4)
    neg_c_softplus = -LRU_C * jax.nn.softplus(-lam_ref[...])
    for blk in range(RNN_BLOCKS):
        cols = slice(blk * RNN_BW, (blk + 1) * RNN_BW)
        r = jax.nn.sigmoid(gates[blk][:, :RNN_BW] + ba_ref[:, cols])
        i = jax.nn.sigmoid(gates[blk][:, RNN_BW:] + bx_ref[:, cols])
        log_a = r * neg_c_softplus[:, cols]
        a = jnp.exp(log_a)
        one_minus_a2 = -jnp.tanh(log_a) * (1.0 + a * a)
        a_buf[:, cols] = a
        b_buf[:, cols] = jnp.sqrt(one_minus_a2) * (i * xr[:, cols])

    gr = proj(1)
    ga = proj(5)
    gb = proj(6)

    hc = h_state[...]
    for t in range(TS):
        rows = slice(t * BATCH, (t + 1) * BATCH)
        hc = a_buf[rows, :] * hc + b_buf[rows, :]
        a_buf[rows, :] = hc
    h_state[...] = hc

    yb = cb_gate * _causal_conv(v_halo, all_cols, v, scw_ref, SHORT_CONV)
    pb = jnp.dot(yb.astype(BF16), wpb_ref[...], preferred_element_type=F32)
    ya = a_buf[...] * jax.nn.gelu(gr)
    pa = jnp.dot(ya.astype(BF16), wpa_ref[...], preferred_element_type=F32)

    m = jax.nn.sigmoid(ga) * pa + jax.nn.sigmoid(gb) * pb
    o_ref[...] = x + jnp.dot(m.astype(BF16), wout_ref[...], preferred_element_type=F32)


def _ffn_kernel(x_ref, p_ref, g_ref, wup_ref, fw_ref, fb_ref, wdn_ref, wpg_ref, wpp_ref,
                gple_ref, gfin_ref, o_ref, up_halo, *, last_layer):
    @pl.when(pl.program_id(0) == 0)
    def _():
        up_halo[...] = jnp.zeros_like(up_halo)

    x = x_ref[...]
    h = _rmsnorm(x, g_ref[...]).astype(BF16)

    n_chunks = D_FF // FF_CHUNK

    def up_dots(c):
        gcols = slice(c * FF_CHUNK, (c + 1) * FF_CHUNK)
        vcols = slice(D_FF + c * FF_CHUNK, D_FF + (c + 1) * FF_CHUNK)
        return [(cols, jnp.dot(h, wup_ref[:, cols], preferred_element_type=F32))
                for cols in (gcols, vcols)]

    def conv_up(cols, pre):
        return _causal_conv(up_halo, cols, pre, fw_ref, FFN_CONV) + fb_ref[:, cols]

    acc = x
    pre = up_dots(0)
    for c in range(n_chunks):
        nxt = up_dots(c + 1) if c + 1 < n_chunks else None
        ug, uv = (conv_up(cols, p_) for cols, p_ in pre)
        act = (jax.nn.gelu(ug) * uv).astype(BF16)
        acc = acc + jnp.dot(act, wdn_ref[c * FF_CHUNK:(c + 1) * FF_CHUNK, :],
                            preferred_element_type=F32)
        pre = nxt
    x = acc

    e = jnp.dot(_to_time_major(p_ref[...]).astype(BF16), wpp_ref[...], preferred_element_type=F32)
    e = _rmsnorm(e, gple_ref[...])
    gate = jax.nn.sigmoid(jnp.dot(x.astype(BF16), wpg_ref[...], preferred_element_type=F32))
    x = x + gate * e
    if last_layer:
        o_ref[...] = _to_batch_major(_rmsnorm(x, gfin_ref[...]))
    else:
        o_ref[...] = x


def _resident(shape):
    nd = len(shape)
    return pl.BlockSpec(shape, lambda i: (0,) * nd, pipeline_mode=pl.Buffered(1))


def _rows(width):
    return pl.BlockSpec((TM, width), lambda i: (i, 0))


def _batch_major(width):
    return pl.BlockSpec((BATCH, TS, width), lambda i: (0, i, 0))


def _to_time_major(a):
    return jnp.swapaxes(a, 0, 1).reshape(TM, a.shape[-1])


def _to_batch_major(a):
    return jnp.swapaxes(a.reshape(TS, BATCH, a.shape[-1]), 0, 1)


_PARAMS = pltpu.CompilerParams(dimension_semantics=("arbitrary",),
                               vmem_limit_bytes=VMEM_LIMIT_BYTES)


def _mix_call(x, g, win, cw, cb, wrg, ba, bx, lam, scw, wpa, wpb, wout):
    batch_major_in = x.ndim == 3
    n_rows = SEQ * BATCH
    consts = (g, win, cw, cb, wrg, ba, bx, lam, scw, wpa, wpb, wout)
    x_spec = _batch_major(D_MODEL) if batch_major_in else _rows(D_MODEL)
    return pl.pallas_call(
        functools.partial(_mix_kernel, batch_major_in=batch_major_in),
        grid=(n_rows // TM,),
        in_specs=[x_spec] + [_resident(c.shape) for c in consts],
        out_specs=_rows(D_MODEL),
        out_shape=jax.ShapeDtypeStruct((n_rows, D_MODEL), F32),
        scratch_shapes=[
            pltpu.VMEM(((RNN_CONV - 1) * BATCH, D_RNN), F32),
            pltpu.VMEM(((SHORT_CONV - 1) * BATCH, D_CONV), F32),
            pltpu.VMEM((BATCH, D_RNN), F32),
            pltpu.VMEM((TM, D_RNN), F32),
            pltpu.VMEM((TM, D_RNN), F32),
        ],
        compiler_params=_PARAMS,
        name="mix",
    )(x, *consts)


def _ffn_call(xt, p, g, wup, fw, fb, wdn, wpg, wpp, gple, gfin, last_layer):
    n_rows = xt.shape[0]
    consts = (g, wup, fw, fb, wdn, wpg, wpp, gple, gfin)
    if last_layer:
        out_spec = _batch_major(D_MODEL)
        out_shape = jax.ShapeDtypeStruct((BATCH, SEQ, D_MODEL), F32)
    else:
        out_spec = _rows(D_MODEL)
        out_shape = jax.ShapeDtypeStruct((n_rows, D_MODEL), F32)
    return pl.pallas_call(
        functools.partial(_ffn_kernel, last_layer=last_layer),
        grid=(n_rows // TM,),
        in_specs=[_rows(D_MODEL), _batch_major(D_PLE)] + [_resident(c.shape) for c in consts],
        out_specs=out_spec,
        out_shape=out_shape,
        scratch_shapes=[pltpu.VMEM(((FFN_CONV - 1) * BATCH, 2 * D_FF), F32)],
        compiler_params=_PARAMS,
        name="ffn",
    )(xt, p, *consts)


def kernel(x, p, g_mix, w_in, rnn_conv_w, rnn_conv_b, w_rg_a, b_rg_a, w_rg_x, b_rg_x, lru_lambda, sc_conv_w, w_proj_a, w_proj_b, w_out, g_ffn, w_up, ffn_conv_w, ffn_conv_b, w_down, w_ple_gate, w_ple_proj, g_ple, g_final):
    depth = w_in.shape[0]
    bsz, seq, d = x.shape
    assert (bsz, seq, d) == (BATCH, SEQ, D_MODEL) and seq % TS == 0

    def row(v):
        return v.reshape(1, -1)

    for l in range(depth):
        w_rg = jnp.concatenate([w_rg_a[l], w_rg_x[l]], axis=-1).astype(BF16)
        x = _mix_call(x, row(g_mix[l]), w_in[l].astype(BF16), rnn_conv_w[l], row(rnn_conv_b[l]),
                      w_rg, row(b_rg_a[l]), row(b_rg_x[l]), row(lru_lambda[l]), sc_conv_w[l],
                      w_proj_a[l].astype(BF16), w_proj_b[l].astype(BF16), w_out[l].astype(BF16))
        x = _ffn_call(x, p[l], row(g_ffn[l]), w_up[l].astype(BF16), ffn_conv_w[l],
                      row(ffn_conv_b[l]), w_down[l].astype(BF16), w_ple_gate[l].astype(BF16),
                      w_ple_proj[l].astype(BF16), row(g_ple[l]), row(g_final),
                      last_layer=(l == depth - 1))
    return x
```

```python
import functools

import jax
import jax.numpy as jnp
from jax import lax
from jax.experimental import pallas as pl
from jax.experimental.pallas import tpu as pltpu

D_MODEL = 1024
BATCH = 8
SEQ = 4096
D_PLE = 256
D_RNN = 1024
RNN_BLOCKS = 8
RNN_BW = D_RNN // RNN_BLOCKS
RNN_CONV = 4
LRU_C = 8.0
D_CONV = 1024
SHORT_CONV = 3
D_FF = 3 * D_MODEL
FFN_CONV = 3
EPS = 1e-6

SUBLANES = 8
assert BATCH == SUBLANES, "time-major layout puts one timestep's batch on the sublanes"

VMEM_LIMIT_BYTES = 60000 * 1024

TS = 64
TM = TS * BATCH
FF_CHUNK = 512
HALF = D_MODEL // 2

F32 = jnp.float32
BF16 = jnp.bfloat16


def _rmsnorm(u, g):
    y = u * lax.rsqrt(jnp.mean(u * u, axis=-1, keepdims=True) + EPS)
    return y * g


def _pack_rows(w):
    wb = w.astype(BF16)
    *lead, k, n = wb.shape
    pairs = jnp.swapaxes(wb.reshape(*lead, k // 2, 2, n), -1, -2)
    return lax.bitcast_convert_type(pairs, jnp.uint32)


def _mm(lhs, packed_rhs):
    return jnp.dot(lhs, pltpu.bitcast(packed_rhs, BF16), preferred_element_type=F32)


def _unforwarded_rows():
    step = pl.program_id(0)
    return (pl.multiple_of(lax.shift_right_logical(step, 30) * BATCH, BATCH),
            pl.multiple_of(lax.shift_right_logical(step, 29) * BATCH, BATCH))


def _causal_conv(halo_ref, cols, pre, w_ref, taps):
    hrows = (taps - 1) * BATCH
    rows = pre.shape[0]
    ext = jnp.concatenate([halo_ref[:, cols], pre], axis=0)
    halo_ref[:, cols] = pre[rows - hrows:, :]
    y = ext[0:rows, :] * w_ref[0:1, cols]
    for k in range(1, taps):
        y = y + ext[k * BATCH:k * BATCH + rows, :] * w_ref[k:k + 1, cols]
    return y


def _mix_kernel(x_ref, g_ref, win_ref, cw_ref, cb_ref, wrg_ref, ba_ref, bx_ref, lam_ref,
                scw_ref, wpa_ref, wpb_ref, wout_ref, o_ref,
                xr_halo, v_halo, h_state, a_buf, b_buf, g_stage, *, batch_major_in):
    @pl.when(pl.program_id(0) == 0)
    def _():
        xr_halo[...] = jnp.zeros_like(xr_halo)
        v_halo[...] = jnp.zeros_like(v_halo)
        h_state[...] = jnp.zeros_like(h_state)

    x = _to_time_major(x_ref[...]) if batch_major_in else x_ref[...]
    h = _rmsnorm(x, g_ref[...]).astype(BF16)

    def proj(k, j):
        c0 = k * D_MODEL + j * HALF
        return _mm(h, win_ref[:, c0:c0 + HALF])

    halves = [slice(j * HALF, (j + 1) * HALF) for j in range(2)]


    xr_pre = [proj(0, j) for j in range(2)]
    cb_gate = [proj(2, j) for j in range(2)]
    cc = [proj(3, j) for j in range(2)]
    xr = [_causal_conv(xr_halo, halves[j], xr_pre[j], cw_ref, RNN_CONV) + cb_ref[:, halves[j]]
          for j in range(2)]
    xr_bf = [u.astype(BF16) for u in xr]
    c_softplus = LRU_C * jax.nn.softplus(-lam_ref[...])

    st_row, ld_row = _unforwarded_rows()

    def gate_dot(blk):
        j, c0 = divmod(blk * RNN_BW, HALF)
        g_stage[pl.ds(st_row, TM), 2 * blk * RNN_BW:2 * (blk + 1) * RNN_BW] = _mm(
            xr_bf[j][:, c0:c0 + RNN_BW], wrg_ref[blk])

    def gate_math(blk):
        cols = slice(blk * RNN_BW, (blk + 1) * RNN_BW)
        j, c0 = divmod(blk * RNN_BW, HALF)
        g0 = 2 * blk * RNN_BW
        r = jax.nn.sigmoid(g_stage[pl.ds(ld_row, TM), g0:g0 + RNN_BW] + ba_ref[:, cols])
        i = jax.nn.sigmoid(g_stage[pl.ds(ld_row, TM), g0 + RNN_BW:g0 + 2 * RNN_BW] + bx_ref[:, cols])
        neg_log_a = r * c_softplus[:, cols]
        a = jnp.exp(-neg_log_a)
        one_minus_a2 = jnp.tanh(neg_log_a) * (1.0 + a * a)
        mult = jnp.where(one_minus_a2 == 0.0, 0.0, one_minus_a2 * lax.rsqrt(one_minus_a2))
        a_buf[:, cols] = a
        b_buf[:, cols] = mult * (i * xr[j][:, c0:c0 + RNN_BW])

    fillers = [4, 1, 5, 6]
    filled = {}
    for pair in range(RNN_BLOCKS // 2):
        blks = (2 * pair, 2 * pair + 1)
        for blk in blks:
            gate_dot(blk)
        filled[fillers[pair]] = [proj(fillers[pair], j) for j in range(2)]
        for blk in blks:
            gate_math(blk)
    cx, gr, ga, gb = (filled[k] for k in (4, 1, 5, 6))

    hc = h_state[...]
    for t in range(TS):
        rows = slice(t * BATCH, (t + 1) * BATCH)
        hc = a_buf[rows, :] * hc + b_buf[rows, :]
        a_buf[rows, :] = hc
    h_state[...] = hc

    yb = [cb_gate[j] * _causal_conv(v_halo, halves[j], cc[j] * cx[j], scw_ref, SHORT_CONV)
          for j in range(2)]
    pb = _mm(jnp.concatenate(yb, axis=1).astype(BF16), wpb_ref[...])
    ya = [a_buf[:, halves[j]] * jax.nn.gelu(gr[j]) for j in range(2)]
    pa = _mm(jnp.concatenate(ya, axis=1).astype(BF16), wpa_ref[...])

    sa = jnp.concatenate([jax.nn.sigmoid(u) for u in ga], axis=1)
    sb = jnp.concatenate([jax.nn.sigmoid(u) for u in gb], axis=1)
    m = sa * pa + sb * pb
    o_ref[...] = x + _mm(m.astype(BF16), wout_ref[...])


def _ffn_kernel(x_ref, p_ref, g_ref, wup_ref, fw_ref, fb_ref, wdn_ref, wpg_ref, wpp_ref,
                gple_ref, gfin_ref, o_ref, up_halo, *, last_layer):
    @pl.when(pl.program_id(0) == 0)
    def _():
        up_halo[...] = jnp.zeros_like(up_halo)

    x = x_ref[...]
    h = _rmsnorm(x, g_ref[...]).astype(BF16)

    n_chunks = D_FF // FF_CHUNK

    def up_dots(c):
        gcols = slice(c * FF_CHUNK, (c + 1) * FF_CHUNK)
        vcols = slice(D_FF + c * FF_CHUNK, D_FF + (c + 1) * FF_CHUNK)
        return [(cols, _mm(h, wup_ref[:, cols])) for cols in (gcols, vcols)]

    def conv_up(cols, pre):
        return _causal_conv(up_halo, cols, pre, fw_ref, FFN_CONV) + fb_ref[:, cols]

    acc = x
    pre = up_dots(0)
    for c in range(n_chunks):
        nxt = up_dots(c + 1) if c + 1 < n_chunks else None
        ug, uv = (conv_up(cols, p_) for cols, p_ in pre)
        act = (jax.nn.gelu(ug) * uv).astype(BF16)
        acc = acc + _mm(act, wdn_ref[c * FF_CHUNK // 2:(c + 1) * FF_CHUNK // 2, :])
        pre = nxt
    x = acc

    e = _mm(_to_time_major(p_ref[...]).astype(BF16), wpp_ref[...])
    e = _rmsnorm(e, gple_ref[...])
    gate = jax.nn.sigmoid(_mm(x.astype(BF16), wpg_ref[...]))
    x = x + gate * e
    if last_layer:
        o_ref[...] = _to_batch_major(_rmsnorm(x, gfin_ref[...]))
    else:
        o_ref[...] = x


def _resident(shape):
    nd = len(shape)
    return pl.BlockSpec(shape, lambda i: (0,) * nd, pipeline_mode=pl.Buffered(1))


def _rows(width):
    return pl.BlockSpec((TM, width), lambda i: (i, 0))


def _batch_major(width):
    return pl.BlockSpec((BATCH, TS, width), lambda i: (0, i, 0))


def _to_time_major(a):
    return jnp.swapaxes(a, 0, 1).reshape(TM, a.shape[-1])


def _to_batch_major(a):
    return jnp.swapaxes(a.reshape(TS, BATCH, a.shape[-1]), 0, 1)


_PARAMS = pltpu.CompilerParams(dimension_semantics=("arbitrary",),
                               vmem_limit_bytes=VMEM_LIMIT_BYTES)


def _mix_call(x, g, win, cw, cb, wrg, ba, bx, lam, scw, wpa, wpb, wout):
    batch_major_in = x.ndim == 3
    n_rows = SEQ * BATCH
    consts = (g, win, cw, cb, wrg, ba, bx, lam, scw, wpa, wpb, wout)
    x_spec = _batch_major(D_MODEL) if batch_major_in else _rows(D_MODEL)
    return pl.pallas_call(
        functools.partial(_mix_kernel, batch_major_in=batch_major_in),
        grid=(n_rows // TM,),
        in_specs=[x_spec] + [_resident(c.shape) for c in consts],
        out_specs=_rows(D_MODEL),
        out_shape=jax.ShapeDtypeStruct((n_rows, D_MODEL), F32),
        scratch_shapes=[
            pltpu.VMEM(((RNN_CONV - 1) * BATCH, D_RNN), F32),
            pltpu.VMEM(((SHORT_CONV - 1) * BATCH, D_CONV), F32),
            pltpu.VMEM((BATCH, D_RNN), F32),
            pltpu.VMEM((TM, D_RNN), F32),
            pltpu.VMEM((TM, D_RNN), F32),
            pltpu.VMEM((TM + BATCH, 2 * D_RNN), F32),
        ],
        compiler_params=_PARAMS,
        name="mix",
    )(x, *consts)


def _ffn_call(xt, p, g, wup, fw, fb, wdn, wpg, wpp, gple, gfin, last_layer):
    n_rows = xt.shape[0]
    consts = (g, wup, fw, fb, wdn, wpg, wpp, gple, gfin)
    if last_layer:
        out_spec = _batch_major(D_MODEL)
        out_shape = jax.ShapeDtypeStruct((BATCH, SEQ, D_MODEL), F32)
    else:
        out_spec = _rows(D_MODEL)
        out_shape = jax.ShapeDtypeStruct((n_rows, D_MODEL), F32)
    return pl.pallas_call(
        functools.partial(_ffn_kernel, last_layer=last_layer),
        grid=(n_rows // TM,),
        in_specs=[_rows(D_MODEL), _batch_major(D_PLE)] + [_resident(c.shape) for c in consts],
        out_specs=out_spec,
        out_shape=out_shape,
        scratch_shapes=[pltpu.VMEM(((FFN_CONV - 1) * BATCH, 2 * D_FF), F32)],
        compiler_params=_PARAMS,
        name="ffn",
    )(xt, p, *consts)


def kernel(x, p, g_mix, w_in, rnn_conv_w, rnn_conv_b, w_rg_a, b_rg_a, w_rg_x, b_rg_x, lru_lambda, sc_conv_w, w_proj_a, w_proj_b, w_out, g_ffn, w_up, ffn_conv_w, ffn_conv_b, w_down, w_ple_gate, w_ple_proj, g_ple, g_final):
    depth = w_in.shape[0]
    bsz, seq, d = x.shape
    assert (bsz, seq, d) == (BATCH, SEQ, D_MODEL) and seq % TS == 0

    def row(v):
        return v.reshape(1, -1)

    for l in range(depth):
        w_rg = _pack_rows(jnp.concatenate([w_rg_a[l], w_rg_x[l]], axis=-1))
        x = _mix_call(x, row(g_mix[l]), _pack_rows(w_in[l]), rnn_conv_w[l], row(rnn_conv_b[l]),
                      w_rg, row(b_rg_a[l]), row(b_rg_x[l]), row(lru_lambda[l]), sc_conv_w[l],
                      _pack_rows(w_proj_a[l]), _pack_rows(w_proj_b[l]), _pack_rows(w_out[l]))
        x = _ffn_call(x, p[l], row(g_ffn[l]), _pack_rows(w_up[l]), ffn_conv_w[l],
                      row(ffn_conv_b[l]), _pack_rows(w_down[l]), _pack_rows(w_ple_gate[l]),
                      _pack_rows(w_ple_proj[l]), row(g_ple[l]), row(g_final),
                      last_layer=(l == depth - 1))
    return x
```

```python
import functools

import jax
import jax.numpy as jnp
from jax import lax
from jax.experimental import pallas as pl
from jax.experimental.pallas import tpu as pltpu

D_MODEL = 1024
BATCH = 8
SEQ = 4096
D_PLE = 256
D_RNN = 1024
RNN_BLOCKS = 8
RNN_BW = D_RNN // RNN_BLOCKS
RNN_CONV = 4
LRU_C = 8.0
D_CONV = 1024
SHORT_CONV = 3
D_FF = 3 * D_MODEL
FFN_CONV = 3
EPS = 1e-6

SUBLANES = 8
assert BATCH == SUBLANES, "time-major layout puts one timestep's batch on the sublanes"

VMEM_LIMIT_BYTES = 60000 * 1024

TS = 64
TM = TS * BATCH
FF_CHUNK = 512
HALF = D_MODEL // 2
PACK_BLOCK_ROWS = 512
PACK_BLOCK_COLS = 1024

F32 = jnp.float32
BF16 = jnp.bfloat16


def _rmsnorm(u, g):
    y = u * lax.rsqrt(jnp.mean(u * u, axis=-1, keepdims=True) + EPS)
    return y * g


def _pack_kernel(*refs):
    n = len(refs) // 2
    for w_ref, o_ref in zip(refs[:n], refs[n:]):
        o_ref[...] = pltpu.bitcast(w_ref[...].astype(BF16), jnp.uint32)


def _pack_rows(*ws):
    k, n = ws[0].shape
    bk, bn = min(k, PACK_BLOCK_ROWS), min(n, PACK_BLOCK_COLS)
    assert k % bk == 0 and n % bn == 0 and all(w.shape == (k, n) for w in ws)
    return pl.pallas_call(
        _pack_kernel,
        grid=(k // bk, n // bn),
        in_specs=[pl.BlockSpec((bk, bn), lambda i, j: (i, j))] * len(ws),
        out_specs=[pl.BlockSpec((bk // 2, bn), lambda i, j: (i, j))] * len(ws),
        out_shape=[jax.ShapeDtypeStruct((k // 2, n), jnp.uint32)] * len(ws),
        name="pack_weights",
    )(*ws)


def _mm(lhs, packed_rhs):
    return jnp.dot(lhs, pltpu.bitcast(packed_rhs, BF16), preferred_element_type=F32)


def _unforwarded_rows():
    step = pl.program_id(0)
    return (pl.multiple_of(lax.shift_right_logical(step, 30) * BATCH, BATCH),
            pl.multiple_of(lax.shift_right_logical(step, 29) * BATCH, BATCH))


def _causal_conv(halo_ref, cols, pre, w_ref, taps):
    hrows = (taps - 1) * BATCH
    rows = pre.shape[0]
    ext = jnp.concatenate([halo_ref[:, cols], pre], axis=0)
    halo_ref[:, cols] = pre[rows - hrows:, :]
    y = ext[0:rows, :] * w_ref[0:1, cols]
    for k in range(1, taps):
        y = y + ext[k * BATCH:k * BATCH + rows, :] * w_ref[k:k + 1, cols]
    return y


def _mix_kernel(x_ref, g_ref, win_ref, cw_ref, cb_ref, wrg_ref, ba_ref, bx_ref, lam_ref,
                scw_ref, wpa_ref, wpb_ref, wout_ref, o_ref,
                xr_halo, v_halo, h_state, a_buf, b_buf, g_stage, *, batch_major_in):
    @pl.when(pl.program_id(0) == 0)
    def _():
        xr_halo[...] = jnp.zeros_like(xr_halo)
        v_halo[...] = jnp.zeros_like(v_halo)
        h_state[...] = jnp.zeros_like(h_state)

    x = _to_time_major(x_ref[...]) if batch_major_in else x_ref[...]
    h = _rmsnorm(x, g_ref[...]).astype(BF16)

    def proj(k, j):
        c0 = k * D_MODEL + j * HALF
        return _mm(h, win_ref[:, c0:c0 + HALF])

    halves = [slice(j * HALF, (j + 1) * HALF) for j in range(2)]


    xr_pre = [proj(0, j) for j in range(2)]
    cb_gate = [proj(2, j) for j in range(2)]
    cc = [proj(3, j) for j in range(2)]
    xr = [_causal_conv(xr_halo, halves[j], xr_pre[j], cw_ref, RNN_CONV) + cb_ref[:, halves[j]]
          for j in range(2)]
    xr_bf = [u.astype(BF16) for u in xr]
    c_softplus = LRU_C * jax.nn.softplus(-lam_ref[...])

    st_row, ld_row = _unforwarded_rows()

    def gate_dot(blk):
        j, c0 = divmod(blk * RNN_BW, HALF)
        g_stage[pl.ds(st_row, TM), 2 * blk * RNN_BW:2 * (blk + 1) * RNN_BW] = _mm(
            xr_bf[j][:, c0:c0 + RNN_BW], wrg_ref[blk])

    def gate_math(blk):
        cols = slice(blk * RNN_BW, (blk + 1) * RNN_BW)
        j, c0 = divmod(blk * RNN_BW, HALF)
        g0 = 2 * blk * RNN_BW
        r = jax.nn.sigmoid(g_stage[pl.ds(ld_row, TM), g0:g0 + RNN_BW] + ba_ref[:, cols])
        i = jax.nn.sigmoid(g_stage[pl.ds(ld_row, TM), g0 + RNN_BW:g0 + 2 * RNN_BW] + bx_ref[:, cols])
        neg_log_a = r * c_softplus[:, cols]
        a = jnp.exp(-neg_log_a)
        one_minus_a2 = jnp.tanh(neg_log_a) * (1.0 + a * a)
        mult = jnp.where(one_minus_a2 == 0.0, 0.0, one_minus_a2 * lax.rsqrt(one_minus_a2))
        a_buf[:, cols] = a
        b_buf[:, cols] = mult * (i * xr[j][:, c0:c0 + RNN_BW])

    fillers = [4, 1, 5, 6]
    filled = {}
    for pair in range(RNN_BLOCKS // 2):
        blks = (2 * pair, 2 * pair + 1)
        for blk in blks:
            gate_dot(blk)
        filled[fillers[pair]] = [proj(fillers[pair], j) for j in range(2)]
        for blk in blks:
            gate_math(blk)
    cx, gr, ga, gb = (filled[k] for k in (4, 1, 5, 6))

    hc = h_state[...]
    for t in range(TS):
        rows = slice(t * BATCH, (t + 1) * BATCH)
        hc = a_buf[rows, :] * hc + b_buf[rows, :]
        a_buf[rows, :] = hc
    h_state[...] = hc

    yb = [cb_gate[j] * _causal_conv(v_halo, halves[j], cc[j] * cx[j], scw_ref, SHORT_CONV)
          for j in range(2)]
    pb = _mm(jnp.concatenate(yb, axis=1).astype(BF16), wpb_ref[...])
    ya = [a_buf[:, halves[j]] * jax.nn.gelu(gr[j]) for j in range(2)]
    pa = _mm(jnp.concatenate(ya, axis=1).astype(BF16), wpa_ref[...])

    sa = jnp.concatenate([jax.nn.sigmoid(u) for u in ga], axis=1)
    sb = jnp.concatenate([jax.nn.sigmoid(u) for u in gb], axis=1)
    m = sa * pa + sb * pb
    o_ref[...] = x + _mm(m.astype(BF16), wout_ref[...])


def _ffn_kernel(x_ref, p_ref, g_ref, wup_ref, fw_ref, fb_ref, wdn_ref, wpg_ref, wpp_ref,
                gple_ref, gfin_ref, o_ref, up_halo, *, last_layer):
    @pl.when(pl.program_id(0) == 0)
    def _():
        up_halo[...] = jnp.zeros_like(up_halo)

    x = x_ref[...]
    h = _rmsnorm(x, g_ref[...]).astype(BF16)

    n_chunks = D_FF // FF_CHUNK

    def up_dots(c):
        gcols = slice(c * FF_CHUNK, (c + 1) * FF_CHUNK)
        vcols = slice(D_FF + c * FF_CHUNK, D_FF + (c + 1) * FF_CHUNK)
        return [(cols, _mm(h, wup_ref[:, cols])) for cols in (gcols, vcols)]

    def conv_up(cols, pre):
        return _causal_conv(up_halo, cols, pre, fw_ref, FFN_CONV) + fb_ref[:, cols]

    acc = x
    pre = up_dots(0)
    for c in range(n_chunks):
        nxt = up_dots(c + 1) if c + 1 < n_chunks else None
        ug, uv = (conv_up(cols, p_) for cols, p_ in pre)
        act = (jax.nn.gelu(ug) * uv).astype(BF16)
        acc = acc + _mm(act, wdn_ref[c * FF_CHUNK // 2:(c + 1) * FF_CHUNK // 2, :])
        pre = nxt
    x = acc

    e = _mm(_to_time_major(p_ref[...]).astype(BF16), wpp_ref[...])
    e = _rmsnorm(e, gple_ref[...])
    gate = jax.nn.sigmoid(_mm(x.astype(BF16), wpg_ref[...]))
    x = x + gate * e
    if last_layer:
        o_ref[...] = _to_batch_major(_rmsnorm(x, gfin_ref[...]))
    else:
        o_ref[...] = x


def _resident(shape):
    nd = len(shape)
    return pl.BlockSpec(shape, lambda i: (0,) * nd, pipeline_mode=pl.Buffered(1))


def _rows(width):
    return pl.BlockSpec((TM, width), lambda i: (i, 0))


def _batch_major(width):
    return pl.BlockSpec((BATCH, TS, width), lambda i: (0, i, 0))


def _to_time_major(a):
    return jnp.swapaxes(a, 0, 1).reshape(TM, a.shape[-1])


def _to_batch_major(a):
    return jnp.swapaxes(a.reshape(TS, BATCH, a.shape[-1]), 0, 1)


_PARAMS = pltpu.CompilerParams(dimension_semantics=("arbitrary",),
                               vmem_limit_bytes=VMEM_LIMIT_BYTES)


def _mix_call(x, g, win, cw, cb, wrg, ba, bx, lam, scw, wpa, wpb, wout):
    batch_major_in = x.ndim == 3
    n_rows = SEQ * BATCH
    consts = (g, win, cw, cb, wrg, ba, bx, lam, scw, wpa, wpb, wout)
    x_spec = _batch_major(D_MODEL) if batch_major_in else _rows(D_MODEL)
    return pl.pallas_call(
        functools.partial(_mix_kernel, batch_major_in=batch_major_in),
        grid=(n_rows // TM,),
        in_specs=[x_spec] + [_resident(c.shape) for c in consts],
        out_specs=_rows(D_MODEL),
        out_shape=jax.ShapeDtypeStruct((n_rows, D_MODEL), F32),
        scratch_shapes=[
            pltpu.VMEM(((RNN_CONV - 1) * BATCH, D_RNN), F32),
            pltpu.VMEM(((SHORT_CONV - 1) * BATCH, D_CONV), F32),
            pltpu.VMEM((BATCH, D_RNN), F32),
            pltpu.VMEM((TM, D_RNN), F32),
            pltpu.VMEM((TM, D_RNN), F32),
            pltpu.VMEM((TM + BATCH, 2 * D_RNN), F32),
        ],
        compiler_params=_PARAMS,
        name="mix",
    )(x, *consts)


def _ffn_call(xt, p, g, wup, fw, fb, wdn, wpg, wpp, gple, gfin, last_layer):
    n_rows = xt.shape[0]
    consts = (g, wup, fw, fb, wdn, wpg, wpp, gple, gfin)
    if last_layer:
        out_spec = _batch_major(D_MODEL)
        out_shape = jax.ShapeDtypeStruct((BATCH, SEQ, D_MODEL), F32)
    else:
        out_spec = _rows(D_MODEL)
        out_shape = jax.ShapeDtypeStruct((n_rows, D_MODEL), F32)
    return pl.pallas_call(
        functools.partial(_ffn_kernel, last_layer=last_layer),
        grid=(n_rows // TM,),
        in_specs=[_rows(D_MODEL), _batch_major(D_PLE)] + [_resident(c.shape) for c in consts],
        out_specs=out_spec,
        out_shape=out_shape,
        scratch_shapes=[pltpu.VMEM(((FFN_CONV - 1) * BATCH, 2 * D_FF), F32)],
        compiler_params=_PARAMS,
        name="ffn",
    )(xt, p, *consts)


def kernel(x, p, g_mix, w_in, rnn_conv_w, rnn_conv_b, w_rg_a, b_rg_a, w_rg_x, b_rg_x, lru_lambda, sc_conv_w, w_proj_a, w_proj_b, w_out, g_ffn, w_up, ffn_conv_w, ffn_conv_b, w_down, w_ple_gate, w_ple_proj, g_ple, g_final):
    depth = w_in.shape[0]
    bsz, seq, d = x.shape
    assert (bsz, seq, d) == (BATCH, SEQ, D_MODEL) and seq % TS == 0

    def row(v):
        return v.reshape(1, -1)

    for l in range(depth):
        w_rg = jnp.concatenate([w_rg_a[l], w_rg_x[l]], axis=-1).reshape(D_RNN, 2 * RNN_BW)
        (w_rg_p,) = _pack_rows(w_rg)
        w_rg_p = w_rg_p.reshape(RNN_BLOCKS, RNN_BW // 2, 2 * RNN_BW)
        (w_in_p,) = _pack_rows(w_in[l])
        (w_up_p,) = _pack_rows(w_up[l])
        (w_down_p,) = _pack_rows(w_down[l])
        (w_ple_proj_p,) = _pack_rows(w_ple_proj[l])
        w_pa_p, w_pb_p, w_out_p, w_ple_gate_p = _pack_rows(
            w_proj_a[l], w_proj_b[l], w_out[l], w_ple_gate[l])
        x = _mix_call(x, row(g_mix[l]), w_in_p, rnn_conv_w[l], row(rnn_conv_b[l]),
                      w_rg_p, row(b_rg_a[l]), row(b_rg_x[l]), row(lru_lambda[l]), sc_conv_w[l],
                      w_pa_p, w_pb_p, w_out_p)
        x = _ffn_call(x, p[l], row(g_ffn[l]), w_up_p, ffn_conv_w[l],
                      row(ffn_conv_b[l]), w_down_p, w_ple_gate_p,
                      w_ple_proj_p, row(g_ple[l]), row(g_final),
                      last_layer=(l == depth - 1))
    return x
```
